```python
import jax, jax.numpy as jnp
from jax import lax
import numpy as np

D_MODEL = 1024
BATCH = 2
SEQ = 8192
DEPTH = 1

RET_HEADS = 4
RET_HEAD_DIM = D_MODEL // 2 // RET_HEADS
RET_WIDTH = RET_HEADS * RET_HEAD_DIM
RET_CHUNK = 128
DN_HEADS = 4
DN_HEAD_DIM = D_MODEL // 2 // DN_HEADS
DN_WIDTH = DN_HEADS * DN_HEAD_DIM
DN_CHUNK = 64
CONV_WIDTH = 4
ROPE_THETA = 10000.0
N_GROUPS = 4
EXPERTS_PER_GROUP = 8
N_EXPERTS = N_GROUPS * EXPERTS_PER_GROUP
TOP_K = 2
D_EXPERT = D_MODEL // 2
N_MOD = 6
NORM_EPS = 1e-6
IN_COLS = 4 * RET_WIDTH + 4 * DN_WIDTH + 2 * DN_HEADS

kernel_name = 'hybrid_retention_gdn_hmoe_adaln'


def rmsnorm(x, w=None):
    xf = x.astype(jnp.float32)
    y = xf * lax.rsqrt(jnp.mean(xf * xf, axis=-1, keepdims=True) + NORM_EPS)
    if w is not None:
        y = y * w.astype(jnp.float32)
    return y.astype(x.dtype)


def l2norm(x):
    return x * lax.rsqrt(jnp.sum(x * x, axis=-1, keepdims=True) + NORM_EPS)


def modulate(h, shift, scale):
    return h * (1.0 + scale[:, None, :]) + shift[:, None, :]


def rotary(x, positions):
    dh = x.shape[-1]
    inv_freq = ROPE_THETA ** (-jnp.arange(0, dh, 2, dtype=jnp.float32) / dh)
    ang = positions.astype(jnp.float32)[..., None] * inv_freq
    cos, sin = jnp.cos(ang)[:, :, None, :], jnp.sin(ang)[:, :, None, :]
    xf = x.astype(jnp.float32)
    x1, x2 = xf[..., : dh // 2], xf[..., dh // 2:]
    return jnp.concatenate([x1 * cos - x2 * sin, x2 * cos + x1 * sin], axis=-1)


def retention(q, k, v):
    B, T, H, Dh = q.shape
    C = RET_CHUNK
    N = T // C
    log_gamma = jnp.log1p(-jnp.power(2.0, -5.0 - jnp.arange(H, dtype=jnp.float32)))
    q = q.astype(jnp.float32).reshape(B, N, C, H, Dh)
    k = (k.astype(jnp.float32) * Dh ** -0.5).reshape(B, N, C, H, Dh)
    v = v.astype(jnp.float32).reshape(B, N, C, H, Dh)
    idx = jnp.arange(C, dtype=jnp.float32)
    rel = idx[:, None] - idx[None, :]
    causal = rel >= 0
    decay = jnp.where(causal[None], jnp.exp(log_gamma[:, None, None] * jnp.where(causal, rel, 0.0)[None]), 0.0)
    scores = jnp.einsum('bnihd,bnjhd->bnhij', q, k) * decay[None, None]
    inner = jnp.einsum('bnhij,bnjhd->bnihd', scores, v)
    k_w = k * jnp.exp(log_gamma[None, :] * (C - 1 - idx)[:, None])[:, :, None]
    incr = jnp.einsum('bnjhk,bnjhv->nbhkv', k_w, v)
    chunk_decay = jnp.exp(log_gamma * C)[None, :, None, None]

    def step(state, incr_n):
        return chunk_decay * state + incr_n, state

    _, s_prev = lax.scan(step, jnp.zeros((B, H, Dh, Dh), jnp.float32), incr)
    q_w = q * jnp.exp(log_gamma[None, :] * (idx + 1.0)[:, None])[:, :, None]
    cross = jnp.einsum('bnihk,nbhkv->bnihv', q_w, s_prev)
    return (inner + cross).reshape(B, T, H, Dh)


def causal_dwconv(x, w):
    k = w.shape[0]
    return lax.conv_general_dilated(x, w[:, None, :].astype(x.dtype), window_strides=(1,), padding=((k - 1, 0),),
                                    dimension_numbers=('NWC', 'WIO', 'NWC'), feature_group_count=x.shape[-1])


def gated_delta_rule(q, k, v, g, beta):
    B, T, H, Dh = q.shape
    C = DN_CHUNK
    N = T // C
    to_chunks = lambda t: t.reshape(B, N, C, H, -1).transpose(0, 3, 1, 2, 4)
    q, k, v = to_chunks(q), to_chunks(k), to_chunks(v)
    g = jnp.cumsum(g.reshape(B, N, C, H).transpose(0, 3, 1, 2), axis=-1)
    beta = beta.reshape(B, N, C, H).transpose(0, 3, 1, 2)
    k_beta = k * beta[..., None]
    v_beta = v * beta[..., None]
    tri_incl = jnp.tril(jnp.ones((C, C), bool))
    tri_strict = jnp.tril(jnp.ones((C, C), bool), -1)
    decay = jnp.exp(jnp.where(tri_incl, g[..., :, None] - g[..., None, :], -jnp.inf))
    a_mat = jnp.where(tri_strict, (k_beta @ jnp.swapaxes(k, -1, -2)) * decay, 0.0)
    eye = jnp.eye(C, dtype=jnp.float32)
    t_mat = lax.linalg.triangular_solve(eye + a_mat, jnp.broadcast_to(eye, a_mat.shape),
                                        left_side=True, lower=True, unit_diagonal=True)
    u = t_mat @ v_beta
    w = t_mat @ (k_beta * jnp.exp(g)[..., None])
    attn = (q @ jnp.swapaxes(k, -1, -2)) * decay
    q_g = q * jnp.exp(g)[..., None]
    k_tail = k * jnp.exp(g[..., -1:] - g)[..., None]
    g_last = jnp.exp(g[..., -1])
    lead = lambda t: jnp.moveaxis(t, 2, 0)

    def step(state, inp):
        u_n, w_n, qg_n, kt_n, attn_n, gl_n = inp
        v_new = u_n - w_n @ state
        o_n = qg_n @ state + attn_n @ v_new
        state = state * gl_n[..., None, None] + jnp.swapaxes(kt_n, -1, -2) @ v_new
        return state, o_n

    _, o = lax.scan(step, jnp.zeros((B, H, Dh, Dh), jnp.float32),
                    (lead(u), lead(w), lead(q_g), lead(k_tail), lead(attn), lead(g_last)))
    return o.transpose(1, 0, 3, 2, 4).reshape(B, T, H, Dh)


def hierarchical_moe(h, w_group, b_group, w_expert, b_expert, w1, w3, w2):
    B, T, D = h.shape
    xf = h.reshape(-1, D)
    n = xf.shape[0]
    group_probs = jax.nn.softmax((xf @ w_group + b_group).astype(jnp.float32), axis=-1)
    p_group, g_idx = lax.top_k(group_probs, 1)
    e_logits = (xf @ w_expert + b_expert).astype(jnp.float32).reshape(n, N_GROUPS, EXPERTS_PER_GROUP)
    e_logits = jnp.take_along_axis(e_logits, g_idx[:, :, None], axis=1)[:, 0]
    p_exp, e_local = lax.top_k(jax.nn.softmax(e_logits, axis=-1), TOP_K)
    gate = p_group * p_exp / jnp.sum(p_exp, axis=-1, keepdims=True)
    flat_e = (g_idx * EXPERTS_PER_GROUP + e_local).reshape(-1)
    order = jnp.argsort(flat_e)
    tok = order // TOP_K
    group_sizes = jnp.bincount(flat_e, length=N_EXPERTS).astype(jnp.int32)
    xs = xf[tok]
    hid = jax.nn.silu(lax.ragged_dot(xs, w1, group_sizes)) * lax.ragged_dot(xs, w3, group_sizes)
    ys = lax.ragged_dot(hid.astype(xs.dtype), w2, group_sizes) * gate.reshape(-1)[order][:, None].astype(xs.dtype)
    return jax.ops.segment_sum(ys, tok, num_segments=n).reshape(B, T, D)


def setup_inputs(seed: int = 0) -> dict:
    key = jax.random.key(seed)
    ks = jax.random.split(key, 24)
    f32 = jnp.float32
    nrm = lambda k, shape, s: jax.random.normal(k, shape, f32) * s
    dt = jnp.exp(jax.random.uniform(ks[9], (DEPTH, DN_HEADS), f32, np.log(1e-3), np.log(1e-1)))
    return {
        'x': nrm(ks[0], (BATCH, SEQ, D_MODEL), 1.0),
        'c': nrm(ks[1], (BATCH, D_MODEL), 1.0),
        'positions': jnp.broadcast_to(jnp.arange(SEQ, dtype=jnp.int32)[None], (BATCH, SEQ)),
        'w_ada': nrm(ks[2], (DEPTH, D_MODEL, N_MOD * D_MODEL), D_MODEL ** -0.5),
        'b_ada': nrm(ks[3], (DEPTH, N_MOD * D_MODEL), 0.01),
        'norm1_w': 1.0 + nrm(ks[4], (DEPTH, D_MODEL), 0.01),
        'w_in': nrm(ks[5], (DEPTH, D_MODEL, IN_COLS), D_MODEL ** -0.5),
        'conv_w': nrm(ks[6], (DEPTH, CONV_WIDTH, 3 * DN_WIDTH), CONV_WIDTH ** -0.5),
        'a_log': jnp.log(jax.random.uniform(ks[7], (DEPTH, DN_HEADS), f32, 1.0, 16.0)),
        'dt_bias': dt + jnp.log(-jnp.expm1(-dt)),
        'dn_norm_w': 1.0 + nrm(ks[8], (DEPTH, DN_HEAD_DIM), 0.01),
        'w_out': nrm(ks[10], (DEPTH, D_MODEL, D_MODEL), D_MODEL ** -0.5),
        'norm2_w': 1.0 + nrm(ks[11], (DEPTH, D_MODEL), 0.01),
        'w_group': nrm(ks[12], (DEPTH, D_MODEL, N_GROUPS), D_MODEL ** -0.5),
        'b_group': nrm(ks[13], (DEPTH, N_GROUPS), 0.01),
        'w_expert': nrm(ks[14], (DEPTH, D_MODEL, N_EXPERTS), D_MODEL ** -0.5),
        'b_expert': nrm(ks[15], (DEPTH, N_EXPERTS), 0.01),
        'w1': nrm(ks[16], (DEPTH, N_EXPERTS, D_MODEL, D_EXPERT), D_MODEL ** -0.5),
        'w3': nrm(ks[17], (DEPTH, N_EXPERTS, D_MODEL, D_EXPERT), D_MODEL ** -0.5),
        'w2': nrm(ks[18], (DEPTH, N_EXPERTS, D_EXPERT, D_MODEL), D_EXPERT ** -0.5),
        'final_norm_w': 1.0 + nrm(ks[19], (D_MODEL,), 0.01),
    }


def reference(x, c, positions, w_ada, b_ada, norm1_w, w_in, conv_w, a_log, dt_bias, dn_norm_w, w_out,
              norm2_w, w_group, b_group, w_expert, b_expert, w1, w3, w2, final_norm_w):
    B, T, D = x.shape
    cuts = [RET_WIDTH, 2 * RET_WIDTH, 3 * RET_WIDTH, 4 * RET_WIDTH,
            4 * RET_WIDTH + DN_WIDTH, 4 * RET_WIDTH + 2 * DN_WIDTH, 4 * RET_WIDTH + 3 * DN_WIDTH,
            4 * RET_WIDTH + 4 * DN_WIDTH, 4 * RET_WIDTH + 4 * DN_WIDTH + DN_HEADS]
    for l in range(DEPTH):
        mod = jax.nn.silu(c) @ w_ada[l] + b_ada[l]
        shift1, scale1, gate1, shift2, scale2, gate2 = jnp.split(mod, N_MOD, axis=-1)

        h = modulate(rmsnorm(x, norm1_w[l]), shift1, scale1)
        proj = h @ w_in[l]
        r_q, r_k, r_v, r_g, d_q, d_k, d_v, d_z, d_a, d_b = jnp.split(proj, cuts, axis=-1)

        heads = lambda t, nh: t.reshape(B, T, nh, -1)
        rq = rotary(heads(r_q, RET_HEADS), positions)
        rk = rotary(heads(r_k, RET_HEADS), positions)
        ret = retention(rq, rk, heads(r_v, RET_HEADS))
        ret = rmsnorm(ret).reshape(B, T, RET_WIDTH) * jax.nn.silu(r_g.astype(jnp.float32))

        qkv = jax.nn.silu(causal_dwconv(jnp.concatenate([d_q, d_k, d_v], axis=-1), conv_w[l]))
        cq, ck, cv = jnp.split(qkv.astype(jnp.float32), 3, axis=-1)
        dq = l2norm(heads(cq, DN_HEADS)) * DN_HEAD_DIM ** -0.5
        dk = l2norm(heads(ck, DN_HEADS))
        beta = jax.nn.sigmoid(d_b.astype(jnp.float32))
        g = -jnp.exp(a_log[l].astype(jnp.float32)) * jax.nn.softplus((d_a + dt_bias[l]).astype(jnp.float32))
        dn = gated_delta_rule(dq, dk, heads(cv, DN_HEADS), g, beta)
        dn = rmsnorm(dn, dn_norm_w[l]).reshape(B, T, DN_WIDTH) * jax.nn.silu(d_z.astype(jnp.float32))

        mix = jnp.concatenate([ret, dn], axis=-1).astype(x.dtype) @ w_out[l]
        x = x + gate1[:, None, :] * mix

        h2 = modulate(rmsnorm(x, norm2_w[l]), shift2, scale2)
        ffn = hierarchical_moe(h2, w_group[l], b_group[l], w_expert[l], b_expert[l], w1[l], w3[l], w2[l])
        x = x + gate2[:, None, :] * ffn.astype(x.dtype)
    return rmsnorm(x, final_norm_w)
```

```python
import functools
import math

import jax
import jax.numpy as jnp
import numpy as np
from jax import lax
from jax.experimental import pallas as pl
from jax.experimental.pallas import tpu as pltpu

F32 = jnp.float32
BF16 = jnp.bfloat16
HIGHEST = lax.Precision.HIGHEST

RET_HEADS = 4
DN_HEADS = 4
HEAD_DIM = 128
RET_WIDTH = RET_HEADS * HEAD_DIM
DN_WIDTH = DN_HEADS * HEAD_DIM
RET_CHUNK = 128
DN_CHUNK = 64
CONV_WIDTH = 4
ROPE_THETA = 10000.0
N_GROUPS = 4
EXPERTS_PER_GROUP = 8
N_EXPERTS = N_GROUPS * EXPERTS_PER_GROUP
N_MOD = 6
NORM_EPS = 1e-6

LANES = 128
SUBLANES = 8
VMEM_LIMIT = 48 * 1024 * 1024

PROJ_MAIN = 4 * RET_WIDTH + 4 * DN_WIDTH
ROUTE_COLS = LANES
EXPERT_COL0 = 8

IN_TM = 512
RET_TB = 512
DN_TB = 256
OUT_TM = 512
GMM_TM = 256
ROWS_CH = 1024
ROWS_WAIT_GROUP = 128
FIN_TM = 256


def _sigmoid(x):
    return 1.0 / (1.0 + jnp.exp(-x))


def _silu(x):
    return x * _sigmoid(x)


def _softplus(x):
    return jnp.maximum(x, 0.0) + jnp.log(1.0 + jnp.exp(-jnp.abs(x)))


def _dot(a, b):
    return jnp.dot(a, b, preferred_element_type=F32)


def _dot_nt(a, b):
    return lax.dot_general(a, b, (((1,), (1,)), ((), ())), preferred_element_type=F32)


def _params(sem):
    return pltpu.CompilerParams(dimension_semantics=sem, vmem_limit_bytes=VMEM_LIMIT)


def _ada_kernel(c_ref, w_ref, b_ref, o_ref):
    s = _silu(c_ref[...])
    o_ref[...] = jnp.dot(s, w_ref[...], precision=HIGHEST, preferred_element_type=F32) + b_ref[...]


def _ada(c_pad, w_ada, b_ada):
    d = c_pad.shape[1]
    ncol = w_ada.shape[1]
    tn = d
    return pl.pallas_call(
        _ada_kernel,
        out_shape=jax.ShapeDtypeStruct((c_pad.shape[0], ncol), F32),
        grid=(ncol // tn,),
        in_specs=[pl.BlockSpec((c_pad.shape[0], d), lambda j: (0, 0)),
                  pl.BlockSpec((d, tn), lambda j: (0, j)),
                  pl.BlockSpec((1, tn), lambda j: (0, j))],
        out_specs=pl.BlockSpec((c_pad.shape[0], tn), lambda j: (0, j)),
        compiler_params=_params(("arbitrary",)),
        name="ada",
    )(c_pad, w_ada, b_ada)


def _inproj_kernel(x_ref, nw_ref, sh_ref, sc_ref, w_ref, wab_ref, o_ref, oab_ref):
    x = x_ref[0]
    ms = jnp.mean(x * x, axis=-1, keepdims=True)
    h = x * lax.rsqrt(ms + NORM_EPS) * nw_ref[...]
    h = h * (1.0 + sc_ref[0]) + sh_ref[0]
    hb = h.astype(BF16)
    ncol = o_ref.shape[2]
    step = 1024
    for j in range(ncol // step):
        o_ref[0, :, j * step:(j + 1) * step] = _dot(hb, w_ref[:, j * step:(j + 1) * step]).astype(o_ref.dtype)
    oab_ref[0] = _dot(hb, wab_ref[...])


def _inproj(x, norm_w, shift, scale, w_main, w_ab):
    b, t, d = x.shape
    tm = min(IN_TM, t)
    return pl.pallas_call(
        _inproj_kernel,
        out_shape=(jax.ShapeDtypeStruct((b, t, PROJ_MAIN), BF16),
                   jax.ShapeDtypeStruct((b, t, LANES), F32)),
        grid=(b, t // tm),
        in_specs=[pl.BlockSpec((1, tm, d), lambda i, j: (i, j, 0)),
                  pl.BlockSpec((1, d), lambda i, j: (0, 0)),
                  pl.BlockSpec((1, 1, d), lambda i, j: (i, 0, 0)),
                  pl.BlockSpec((1, 1, d), lambda i, j: (i, 0, 0)),
                  pl.BlockSpec((d, PROJ_MAIN), lambda i, j: (0, 0)),
                  pl.BlockSpec((d, LANES), lambda i, j: (0, 0))],
        out_specs=(pl.BlockSpec((1, tm, PROJ_MAIN), lambda i, j: (i, j, 0)),
                   pl.BlockSpec((1, tm, LANES), lambda i, j: (i, j, 0))),
        compiler_params=_params(("parallel", "arbitrary")),
        name="inproj",
    )(x, norm_w, shift, scale, w_main, w_ab)


def _ret_kernel(pos_ref, q_ref, k_ref, v_ref, g_ref, o_ref, state_ref):
    tb = q_ref.shape[1]
    c = RET_CHUNK
    dh = HEAD_DIM

    @pl.when(pl.program_id(1) == 0)
    def _():
        state_ref[...] = jnp.zeros_like(state_ref)

    pos = pos_ref[0].astype(F32)
    pos_t = jnp.transpose(jnp.broadcast_to(pos, (LANES, tb)))
    lane = lax.broadcasted_iota(jnp.int32, (1, dh), 1)
    half = dh // 2
    fidx = jnp.where(lane >= half, lane - half, lane).astype(F32)
    inv_freq = jnp.power(jnp.float32(ROPE_THETA), -(2.0 * fidx) / dh)
    ang = pos_t * inv_freq
    cos = jnp.cos(ang)
    sin = jnp.where(lane >= half, 1.0, -1.0) * jnp.sin(ang)

    ri = lax.broadcasted_iota(jnp.int32, (c, c), 0)
    ci = lax.broadcasted_iota(jnp.int32, (c, c), 1)
    rel = (ri - ci).astype(F32)
    causal = ri >= ci
    row_f = lax.broadcasted_iota(jnp.int32, (c, dh), 0).astype(F32)

    for h in range(RET_HEADS):
        log_gamma = math.log1p(-(2.0 ** (-5.0 - h)))
        decay = jnp.where(causal, jnp.exp(log_gamma * jnp.where(causal, rel, 0.0)), 0.0)
        q_dec = jnp.exp(log_gamma * (row_f + 1.0))
        k_dec = jnp.exp(log_gamma * (c - 1.0 - row_f))
        chunk_decay = math.exp(log_gamma * c)
        sl = slice(h * dh, (h + 1) * dh)
        qh = q_ref[0, :, sl].astype(F32)
        kh = k_ref[0, :, sl].astype(F32)
        qh = qh * cos + pltpu.roll(qh, half, axis=1) * sin
        kh = (kh * cos + pltpu.roll(kh, half, axis=1) * sin) * (dh ** -0.5)
        state = state_ref[h]
        for n in range(tb // c):
            rs = slice(n * c, (n + 1) * c)
            qc = qh[rs]
            kc = kh[rs]
            vc = v_ref[0, rs, sl]
            scores = _dot_nt(qc.astype(BF16), kc.astype(BF16)) * decay
            inner = _dot(scores.astype(BF16), vc)
            cross = _dot((qc * q_dec).astype(BF16), state.astype(BF16))
            kw = (kc * k_dec).astype(BF16)
            incr = _dot(jnp.transpose(kw.astype(F32)).astype(BF16), vc)
            state = chunk_decay * state + incr
            o = inner + cross
            o = o * lax.rsqrt(jnp.mean(o * o, axis=-1, keepdims=True) + NORM_EPS)
            gate = g_ref[0, rs, sl].astype(F32)
            o_ref[0, rs, sl] = (o * _silu(gate)).astype(o_ref.dtype)
        state_ref[h] = state


def _retention(pos3, proj):
    b, t, _ = proj.shape
    tb = min(RET_TB, t)
    w = RET_WIDTH
    spec = lambda col: pl.BlockSpec((1, tb, w), lambda i, j, col=col: (i, j, col))
    return pl.pallas_call(
        _ret_kernel,
        out_shape=jax.ShapeDtypeStruct((b, t, w), BF16),
        grid=(b, t // tb),
        in_specs=[pl.BlockSpec((1, 1, tb), lambda i, j: (i, 0, j)), spec(0), spec(1), spec(2), spec(3)],
        out_specs=pl.BlockSpec((1, tb, w), lambda i, j: (i, j, 0)),
        scratch_shapes=[pltpu.VMEM((RET_HEADS, HEAD_DIM, HEAD_DIM), F32)],
        compiler_params=_params(("parallel", "arbitrary")),
        name="retention",
    )(pos3, proj, proj, proj, proj)


def _dn_kernel(q_ref, k_ref, v_ref, z_ref, ab_ref, cw_ref, alog_ref, dtb_ref, nw_ref, o_ref,
               state_ref, tail_ref):
    tb = q_ref.shape[1]
    c = DN_CHUNK
    dh = HEAD_DIM
    nh = DN_HEADS
    w = DN_WIDTH
    sr = nh * c

    @pl.when(pl.program_id(1) == 0)
    def _():
        state_ref[...] = jnp.zeros_like(state_ref)
        tail_ref[...] = jnp.zeros_like(tail_ref)

    def conv(idx, x_ref):
        x = x_ref[0].astype(F32)
        xp = jnp.concatenate([tail_ref[idx], x], axis=0)
        cw = cw_ref[:, idx * w:(idx + 1) * w]
        y = x * cw[CONV_WIDTH - 1:CONV_WIDTH]
        for j in range(1, CONV_WIDTH):
            xs = pltpu.roll(xp, j, axis=0)[SUBLANES:]
            y = y + xs * cw[CONV_WIDTH - 1 - j:CONV_WIDTH - j]
        tail_ref[idx] = x[tb - SUBLANES:]
        return _silu(y)

    cq = conv(0, q_ref)
    ck = conv(1, k_ref)
    cv = conv(2, v_ref)

    ab = ab_ref[0]
    g_all = -jnp.exp(alog_ref[...]) * _softplus(ab + dtb_ref[...])
    beta_all = _sigmoid(ab)

    r = lax.broadcasted_iota(jnp.int32, (sr, sr), 0)
    cc = lax.broadcasted_iota(jnp.int32, (sr, sr), 1)
    same = (r >> 6) == (cc >> 6)
    incl = jnp.logical_and(same, r >= cc)
    strict = jnp.logical_and(same, r > cc)
    upper = jnp.logical_and(same, cc > r)
    blk16 = (r >> 4) == (cc >> 4)
    blk32 = (r >> 5) == (cc >> 5)
    eye = (r == cc).astype(F32)
    l_incl = jnp.where(incl, 1.0, 0.0)
    u_strict = jnp.where(upper, 1.0, 0.0)

    def stack(x, n):
        return jnp.concatenate([x[n * c:(n + 1) * c, h * dh:(h + 1) * dh] for h in range(nh)], axis=0)

    def stack_col(x, n, lane0):
        return jnp.concatenate([x[n * c:(n + 1) * c, lane0 + h:lane0 + h + 1] for h in range(nh)], axis=0)

    for n in range(tb // c):
        qs = stack(cq, n)
        ks = stack(ck, n)
        vs = stack(cv, n)
        qs = qs * lax.rsqrt(jnp.sum(qs * qs, axis=-1, keepdims=True) + NORM_EPS) * (dh ** -0.5)
        ks = ks * lax.rsqrt(jnp.sum(ks * ks, axis=-1, keepdims=True) + NORM_EPS)
        g_b = jnp.broadcast_to(stack_col(g_all, n, 0), (sr, dh))
        beta_b = jnp.broadcast_to(stack_col(beta_all, n, nh), (sr, dh))

        gc = jnp.dot(l_incl, g_b, precision=HIGHEST, preferred_element_type=F32)
        g_rest = jnp.dot(u_strict, g_b, precision=HIGHEST, preferred_element_type=F32)
        gc_t = jnp.transpose(gc)
        gc_row = jnp.concatenate([gc_t, gc_t], axis=0)
        gc_col = jnp.concatenate([gc, gc], axis=1)
        decay = jnp.where(incl, jnp.exp(jnp.where(incl, gc_col - gc_row, 0.0)), 0.0)

        kb = ks * beta_b
        ksb = ks.astype(BF16)
        a_mat = jnp.where(strict, _dot_nt(kb.astype(BF16), ksb) * decay, 0.0)
        attn = _dot_nt(qs.astype(BF16), ksb) * decay

        d16 = jnp.where(blk16, a_mat, 0.0)
        d1 = d16.astype(BF16)
        d2 = _dot(d1, d1)
        d2b = d2.astype(BF16)
        d4 = _dot(d2b, d2b)
        d4b = d4.astype(BF16)
        d8b = _dot(d4b, d4b).astype(BF16)
        p = eye - d16
        p = p + _dot(p.astype(BF16), d2b)
        p = p + _dot(p.astype(BF16), d4b)
        p = p + _dot(p.astype(BF16), d8b)
        off32 = jnp.where(jnp.logical_and(blk32, jnp.logical_not(blk16)), a_mat, 0.0).astype(BF16)
        pb = p.astype(BF16)
        p = p - _dot(pb, _dot(off32, pb).astype(BF16))
        off64 = jnp.where(blk32, 0.0, a_mat).astype(BF16)
        pb = p.astype(BF16)
        t_mat = p - _dot(pb, _dot(off64, pb).astype(BF16))

        e_gc = jnp.exp(gc)
        rhs = jnp.concatenate([vs * beta_b, kb * e_gc], axis=1).astype(BF16)
        uw = _dot(t_mat.astype(BF16), rhs)
        u = uw[:, :dh]
        wm = uw[:, dh:].astype(BF16)
        qg = (qs * e_gc).astype(BF16)
        kt = (ks * jnp.exp(g_rest)).astype(BF16)

        v_new = []
        o1 = []
        for h in range(nh):
            hs = slice(h * c, (h + 1) * c)
            state = state_ref[h]
            sb = state.astype(BF16)
            vn = u[hs] - _dot(wm[hs], sb)
            o1.append(_dot(qg[hs], sb))
            g_last = jnp.exp(gc[h * c + c - 1:h * c + c, :])
            kt_t = jnp.transpose(jnp.concatenate([kt[hs].astype(F32), jnp.zeros((c, dh), F32)], axis=0))
            vn_pad = jnp.concatenate([vn, jnp.zeros((c, dh), F32)], axis=0)
            state_ref[h] = state * g_last + _dot(kt_t.astype(BF16), vn_pad.astype(BF16))
            v_new.append(vn)
        o2 = _dot(attn.astype(BF16), jnp.concatenate(v_new, axis=0).astype(BF16))
        for h in range(nh):
            o = o1[h] + o2[h * c:(h + 1) * c]
            o = o * lax.rsqrt(jnp.mean(o * o, axis=-1, keepdims=True) + NORM_EPS) * nw_ref[...]
            z = z_ref[0, n * c:(n + 1) * c, h * dh:(h + 1) * dh].astype(F32)
            o_ref[0, n * c:(n + 1) * c, h * dh:(h + 1) * dh] = (o * _silu(z)).astype(o_ref.dtype)


def _deltanet(proj, proj_ab, conv_w, alog_row, dtb_row, dn_norm_w):
    b, t, _ = proj.shape
    tb = min(DN_TB, t)
    w = DN_WIDTH
    spec = lambda col: pl.BlockSpec((1, tb, w), lambda i, j, col=col: (i, j, col))
    const = lambda shape: pl.BlockSpec(shape, lambda i, j: (0,) * len(shape))
    return pl.pallas_call(
        _dn_kernel,
        out_shape=jax.ShapeDtypeStruct((b, t, w), BF16),
        grid=(b, t // tb),
        in_specs=[spec(4), spec(5), spec(6), spec(7),
                  pl.BlockSpec((1, tb, LANES), lambda i, j: (i, j, 0)),
                  const((CONV_WIDTH, 3 * w)), const((1, LANES)), const((1, LANES)), const((1, HEAD_DIM))],
        out_specs=pl.BlockSpec((1, tb, w), lambda i, j: (i, j, 0)),
        scratch_shapes=[pltpu.VMEM((DN_HEADS, HEAD_DIM, HEAD_DIM), F32),
                        pltpu.VMEM((3, SUBLANES, w), F32)],
        compiler_params=_params(("parallel", "arbitrary")),
        name="deltanet",
    )(proj, proj, proj, proj, proj_ab, conv_w, alog_row, dtb_row, dn_norm_w)


def _outproj_kernel(ret_ref, dn_ref, x_ref, wo_ref, g1_ref, nw_ref, sh_ref, sc_ref, wr_ref, br_ref,
                    x1_ref, h2_ref, ri_ref, rg_ref, gcol_ref, cnt_ref, carry_ref):
    tm = x_ref.shape[1]
    d = x_ref.shape[2]
    half = wo_ref.shape[0] // 2

    @pl.when(jnp.logical_and(pl.program_id(0) == 0, pl.program_id(1) == 0))
    def _():
        carry_ref[...] = jnp.zeros_like(carry_ref)

    mix = _dot(ret_ref[0], wo_ref[:half]) + _dot(dn_ref[0], wo_ref[half:])
    x1 = x_ref[0] + g1_ref[0] * mix
    x1_ref[0] = x1
    ms = jnp.mean(x1 * x1, axis=-1, keepdims=True)
    h2 = x1 * lax.rsqrt(ms + NORM_EPS) * nw_ref[...]
    h2 = h2 * (1.0 + sc_ref[0]) + sh_ref[0]
    for j in range(d // LANES):
        h2_ref[pl.ds(j, tm, stride=SUBLANES), :] = h2[:, j * LANES:(j + 1) * LANES]

    logits = jnp.dot(h2, wr_ref[...], precision=HIGHEST, preferred_element_type=F32) + br_ref[...]
    lt = jnp.transpose(logits)
    sub = lax.broadcasted_iota(jnp.int32, (SUBLANES, tm), 0).astype(F32)
    neg = jnp.float32(-1e30)

    lg = jnp.where(sub < N_GROUPS, lt[0:SUBLANES], neg)
    mg = jnp.max(lg, axis=0, keepdims=True)
    gi = jnp.min(jnp.where(lg == mg, sub, SUBLANES), axis=0, keepdims=True)
    p_group = 1.0 / jnp.sum(jnp.exp(lg - mg), axis=0, keepdims=True)

    le = jnp.zeros((SUBLANES, tm), F32)
    for g in range(N_GROUPS):
        le = le + jnp.where(gi == g, lt[EXPERT_COL0 + g * SUBLANES:EXPERT_COL0 + (g + 1) * SUBLANES], 0.0)
    m1 = jnp.max(le, axis=0, keepdims=True)
    i1 = jnp.min(jnp.where(le == m1, sub, SUBLANES), axis=0, keepdims=True)
    le2 = jnp.where(sub == i1, neg, le)
    m2 = jnp.max(le2, axis=0, keepdims=True)
    i2 = jnp.min(jnp.where(le2 == m2, sub, SUBLANES), axis=0, keepdims=True)
    e2 = jnp.exp(m2 - m1)
    gate0 = p_group / (1.0 + e2)
    gate1 = p_group * e2 / (1.0 + e2)
    ex0 = gi * EXPERTS_PER_GROUP + i1
    ex1 = gi * EXPERTS_PER_GROUP + i2

    eid = lax.broadcasted_iota(jnp.int32, (N_EXPERTS, tm), 0).astype(F32)
    oh0 = jnp.where(eid == ex0, 1.0, 0.0)
    oh1 = jnp.where(eid == ex1, 1.0, 0.0)
    oh = oh0 + oh1
    tr = lax.broadcasted_iota(jnp.int32, (tm, tm), 0)
    tc = lax.broadcasted_iota(jnp.int32, (tm, tm), 1)
    upper_incl = jnp.where(tr <= tc, 1.0, 0.0).astype(BF16)
    before = _dot(oh.astype(BF16), upper_incl) - oh + carry_ref[...]
    rank0 = jnp.sum(oh0 * before, axis=0, keepdims=True)
    rank1 = jnp.sum(oh1 * before, axis=0, keepdims=True)
    carry = carry_ref[...] + jnp.sum(oh, axis=1, keepdims=True)
    carry_ref[...] = carry
    cnt_ref[...] = carry[:, 0:LANES]

    zf = jnp.zeros((1, tm), F32)
    ri_ref[...] = jnp.concatenate([ex0, ex1, rank0, rank1, zf, zf, zf, zf], axis=0).astype(jnp.int32)
    gates = jnp.concatenate([gate0, gate1, zf, zf, zf, zf, zf, zf], axis=0)
    rg_ref[...] = gates
    gpad = jnp.concatenate([gates, jnp.zeros((LANES - SUBLANES, tm), F32)], axis=0)
    gcol_ref[...] = jnp.transpose(gpad)


def _outproj(ret, dn, x, w_out, gate1, norm_w, shift, scale, w_route, b_route):
    b, t, d = x.shape
    tm = min(OUT_TM, t)
    n = b * t
    nt = t // tm
    const = lambda shape: pl.BlockSpec(shape, lambda i, j: (0,) * len(shape))
    per_b = pl.BlockSpec((1, 1, d), lambda i, j: (i, 0, 0))
    return pl.pallas_call(
        _outproj_kernel,
        out_shape=(jax.ShapeDtypeStruct((b, t, d), F32),
                   jax.ShapeDtypeStruct((n * SUBLANES, LANES), F32),
                   jax.ShapeDtypeStruct((SUBLANES, n), jnp.int32),
                   jax.ShapeDtypeStruct((SUBLANES, n), F32),
                   jax.ShapeDtypeStruct((n, LANES), F32),
                   jax.ShapeDtypeStruct((N_EXPERTS, LANES), F32)),
        grid=(b, nt),
        in_specs=[pl.BlockSpec((1, tm, RET_WIDTH), lambda i, j: (i, j, 0)),
                  pl.BlockSpec((1, tm, DN_WIDTH), lambda i, j: (i, j, 0)),
                  pl.BlockSpec((1, tm, d), lambda i, j: (i, j, 0)),
                  const(w_out.shape), per_b, const((1, d)), per_b, per_b,
                  const((d, ROUTE_COLS)), const((1, ROUTE_COLS))],
        out_specs=(pl.BlockSpec((1, tm, d), lambda i, j: (i, j, 0)),
                   pl.BlockSpec((tm * SUBLANES, LANES), lambda i, j: (i * nt + j, 0)),
                   pl.BlockSpec((SUBLANES, tm), lambda i, j: (0, i * nt + j)),
                   pl.BlockSpec((SUBLANES, tm), lambda i, j: (0, i * nt + j)),
                   pl.BlockSpec((tm, LANES), lambda i, j: (i * nt + j, 0)),
                   const((N_EXPERTS, LANES))),
        scratch_shapes=[pltpu.VMEM((N_EXPERTS, tm), F32)],
        compiler_params=_params(("arbitrary", "arbitrary")),
        name="outproj_router",
    )(ret, dn, x, w_out, gate1, norm_w, shift, scale, w_route, b_route)


def _rows_kernel(sidx_ref, didx_ref, src_ref, zero_ref, dst_ref, sem, *, copy_steps):
    ch = sidx_ref.shape[2]

    def dst_tile(i):
        return dst_ref.at[pl.ds(didx_ref[0, 0, i] * SUBLANES, SUBLANES)]

    @pl.when(pl.program_id(0) < copy_steps)
    def _():
        def body(i, carry):
            src_tile = src_ref.at[pl.ds(sidx_ref[0, 0, i] * SUBLANES, SUBLANES)]
            pltpu.make_async_copy(src_tile, dst_tile(i), sem).start()
            return carry
        lax.fori_loop(0, ch, body, 0)

    @pl.when(pl.program_id(0) >= copy_steps)
    def _():
        def body(i, carry):
            pltpu.make_async_copy(zero_ref, dst_tile(i), sem).start()
            return carry
        lax.fori_loop(0, ch, body, 0)

    grp = min(ROWS_WAIT_GROUP, ch)
    for _ in range(ch // grp):
        pltpu.make_async_copy(dst_ref.at[pl.ds(0, grp * SUBLANES)],
                              dst_ref.at[pl.ds(0, grp * SUBLANES)], sem).wait()


def _permute_rows(src_idx, dst_idx, src, zero_dst=None):
    n_copy = src_idx.shape[0]
    n_zero = 0 if zero_dst is None else zero_dst.shape[0]
    ch = min(ROWS_CH, n_copy)
    if n_zero:
        dst_idx = jnp.concatenate([dst_idx, zero_dst])
        src_idx = jnp.concatenate([src_idx, jnp.zeros((n_zero,), jnp.int32)])
    steps = (n_copy + n_zero) // ch
    s3 = src_idx.reshape(steps, 1, ch)
    d3 = dst_idx.reshape(steps, 1, ch)
    zero = jnp.zeros((SUBLANES, LANES), src.dtype)
    smem = pl.BlockSpec((1, 1, ch), lambda i: (i, 0, 0), memory_space=pltpu.SMEM)
    return pl.pallas_call(
        functools.partial(_rows_kernel, copy_steps=n_copy // ch),
        out_shape=jax.ShapeDtypeStruct(((n_copy + n_zero) * SUBLANES, LANES), src.dtype),
        grid=(steps,),
        in_specs=[smem, smem, pl.BlockSpec(memory_space=pl.ANY), pl.BlockSpec(memory_space=pl.ANY)],
        out_specs=pl.BlockSpec(memory_space=pl.ANY),
        scratch_shapes=[pltpu.SemaphoreType.DMA],
        compiler_params=_params(("arbitrary",)),
        name="permute_rows",
    )(s3, d3, src, zero)


def _gmm_kernel(te_ref, x_ref, w1_ref, w3_ref, w2_ref, y_ref):
    tm = x_ref.shape[0] // SUBLANES
    d = w1_ref.shape[1]
    nblk = d // LANES
    x = jnp.concatenate([x_ref[pl.ds(j, tm, stride=SUBLANES), :] for j in range(nblk)], axis=1).astype(BF16)
    h1 = _dot(x, w1_ref[0])
    h3 = _dot(x, w3_ref[0])
    hid = (_silu(h1) * h3).astype(BF16)
    y = _dot(hid, w2_ref[0])
    for j in range(nblk):
        y_ref[pl.ds(j, tm, stride=SUBLANES), :] = y[:, j * LANES:(j + 1) * LANES]


def _gmm(tile_expert, xs, w1, w3, w2):
    n_rows = xs.shape[0] // SUBLANES
    tm = GMM_TM
    ne, d, de = w1.shape
    return pl.pallas_call(
        _gmm_kernel,
        out_shape=jax.ShapeDtypeStruct(xs.shape, F32),
        grid_spec=pltpu.PrefetchScalarGridSpec(
            num_scalar_prefetch=1,
            grid=(n_rows // tm,),
            in_specs=[pl.BlockSpec((tm * SUBLANES, LANES), lambda i, te: (i, 0)),
                      pl.BlockSpec((1, d, de), lambda i, te: (te[i], 0, 0)),
                      pl.BlockSpec((1, d, de), lambda i, te: (te[i], 0, 0)),
                      pl.BlockSpec((1, de, d), lambda i, te: (te[i], 0, 0))],
            out_specs=pl.BlockSpec((tm * SUBLANES, LANES), lambda i, te: (i, 0))),
        compiler_params=_params(("arbitrary",)),
        name="gmm",
    )(tile_expert, xs, w1, w3, w2)


def _final_kernel(x1_ref, y_ref, gcol_ref, g2_ref, fw_ref, o_ref, *, normalize):
    tm = x1_ref.shape[1]
    d = x1_ref.shape[2]
    nblk = d // LANES
    y0 = jnp.concatenate([y_ref[pl.ds(j, tm, stride=2 * SUBLANES), :] for j in range(nblk)], axis=1)
    y1 = jnp.concatenate([y_ref[pl.ds(SUBLANES + j, tm, stride=2 * SUBLANES), :] for j in range(nblk)], axis=1)
    gc = gcol_ref[...]
    ffn = y0 * gc[:, 0:1] + y1 * gc[:, 1:2]
    x2 = x1_ref[0] + g2_ref[0] * ffn
    if normalize:
        x2 = x2 * lax.rsqrt(jnp.mean(x2 * x2, axis=-1, keepdims=True) + NORM_EPS) * fw_ref[...]
    o_ref[0] = x2


def _final(x1, y2, gcol, gate2, final_w, normalize):
    b, t, d = x1.shape
    tm = min(FIN_TM, t)
    nt = t // tm
    return pl.pallas_call(
        functools.partial(_final_kernel, normalize=normalize),
        out_shape=jax.ShapeDtypeStruct((b, t, d), F32),
        grid=(b, nt),
        in_specs=[pl.BlockSpec((1, tm, d), lambda i, j: (i, j, 0)),
                  pl.BlockSpec((tm * 2 * SUBLANES, LANES), lambda i, j: (i * nt + j, 0)),
                  pl.BlockSpec((tm, LANES), lambda i, j: (i * nt + j, 0)),
                  pl.BlockSpec((1, 1, d), lambda i, j: (i, 0, 0)),
                  pl.BlockSpec((1, d), lambda i, j: (0, 0))],
        out_specs=pl.BlockSpec((1, tm, d), lambda i, j: (i, j, 0)),
        compiler_params=_params(("parallel", "arbitrary")),
        name="final",
    )(x1, y2, gcol, gate2, final_w)


def _pad_lanes(v, lane0=0):
    return jnp.zeros((1, LANES), F32).at[0, lane0:lane0 + v.shape[0]].set(v.astype(F32))


def kernel(x, c, positions, w_ada, b_ada, norm1_w, w_in, conv_w, a_log, dt_bias, dn_norm_w, w_out,
           norm2_w, w_group, b_group, w_expert, b_expert, w1, w3, w2, final_norm_w):
    b, t, d = x.shape
    n = b * t
    depth = w_ada.shape[0]
    pos3 = positions.reshape(b, 1, t)
    c_pad = jnp.zeros((SUBLANES, d), F32).at[:b].set(c)

    for l in range(depth):
        mod = _ada(c_pad, w_ada[l], b_ada[l].reshape(1, -1))[:b]
        shift1, scale1, gate1, shift2, scale2, gate2 = [m.reshape(b, 1, d) for m in jnp.split(mod, N_MOD, axis=-1)]

        w_main = w_in[l][:, :PROJ_MAIN].astype(BF16)
        w_ab = jnp.zeros((d, LANES), F32).at[:, :2 * DN_HEADS].set(w_in[l][:, PROJ_MAIN:]).astype(BF16)
        proj, proj_ab = _inproj(x, norm1_w[l].reshape(1, d), shift1, scale1, w_main, w_ab)

        ret = _retention(pos3, proj)
        dn = _deltanet(proj, proj_ab, conv_w[l], _pad_lanes(a_log[l]), _pad_lanes(dt_bias[l]),
                       dn_norm_w[l].reshape(1, HEAD_DIM))

        w_route = (jnp.zeros((d, ROUTE_COLS), F32).at[:, :N_GROUPS].set(w_group[l])
                   .at[:, EXPERT_COL0:EXPERT_COL0 + N_EXPERTS].set(w_expert[l]))
        b_route = (jnp.zeros((1, ROUTE_COLS), F32).at[0, :N_GROUPS].set(b_group[l])
                   .at[0, EXPERT_COL0:EXPERT_COL0 + N_EXPERTS].set(b_expert[l]))
        x1, h2r, ri, _, gcol, cnt = _outproj(ret, dn, x, w_out[l].astype(BF16), gate1, norm2_w[l].reshape(1, d),
                                             shift2, scale2, w_route, b_route)

        counts = cnt[:, 0].astype(jnp.int32)
        padded = ((counts + GMM_TM - 1) // GMM_TM) * GMM_TM
        ends = jnp.cumsum(padded)
        starts = ends - padded
        n_pad = N_EXPERTS * GMM_TM
        n_rows = 2 * n + n_pad
        pos = jnp.concatenate([starts[ri[0]] + ri[2], starts[ri[1]] + ri[3]])
        tok = jnp.concatenate([jnp.arange(n, dtype=jnp.int32)] * 2)
        pad_cnt = (padded - counts).at[N_EXPERTS - 1].add(n_rows - ends[N_EXPERTS - 1])
        pad_end = jnp.cumsum(pad_cnt)
        q = jnp.arange(n_pad, dtype=jnp.int32)
        q_exp = jnp.sum(q[:, None] >= pad_end[None, :], axis=1)
        pad_dst = (starts + counts)[q_exp] + q - (pad_end - pad_cnt)[q_exp]
        xs = _permute_rows(tok, pos, h2r, zero_dst=pad_dst.astype(jnp.int32))

        tile_start = jnp.arange(n_rows // GMM_TM, dtype=jnp.int32) * GMM_TM
        tile_expert = jnp.minimum(jnp.sum(tile_start[:, None] >= ends[None, :], axis=1), N_EXPERTS - 1).astype(jnp.int32)
        ys = _gmm(tile_expert, xs, w1[l].astype(BF16), w3[l].astype(BF16), w2[l].astype(BF16))

        slot = jnp.concatenate([jnp.zeros((n,), jnp.int32), jnp.ones((n,), jnp.int32)])
        y2 = _permute_rows(pos, 2 * tok + slot, ys)
        x = _final(x1, y2, gcol, gate2, final_norm_w.reshape(1, d), normalize=(l == depth - 1))
    return x
```

```python
import functools
import math

import jax
import jax.numpy as jnp
import numpy as np
from jax import lax
from jax.experimental import pallas as pl
from jax.experimental.pallas import tpu as pltpu

F32 = jnp.float32
BF16 = jnp.bfloat16
HIGHEST = lax.Precision.HIGHEST

RET_HEADS = 4
DN_HEADS = 4
HEAD_DIM = 128
RET_WIDTH = RET_HEADS * HEAD_DIM
DN_WIDTH = DN_HEADS * HEAD_DIM
RET_CHUNK = 128
DN_CHUNK = 64
CONV_WIDTH = 4
ROPE_THETA = 10000.0
N_GROUPS = 4
EXPERTS_PER_GROUP = 8
N_EXPERTS = N_GROUPS * EXPERTS_PER_GROUP
N_MOD = 6
NORM_EPS = 1e-6

LANES = 128
SUBLANES = 8
VMEM_LIMIT = 48 * 1024 * 1024

PROJ_MAIN = 4 * RET_WIDTH + 4 * DN_WIDTH
ROUTE_COLS = LANES
EXPERT_COL0 = 8

IN_TM = 512
RET_TB = 512
DN_TB = 256
OUT_TM = 512
GMM_TM = 256
ROWS_CH = 1024
ROWS_WAIT_GROUP = 128
ROWS_UNROLL = 8
FIN_TM = 256


def _sigmoid(x):
    return 1.0 / (1.0 + jnp.exp(-x))


def _silu(x):
    return x * _sigmoid(x)


def _softplus(x):
    return jnp.maximum(x, 0.0) + jnp.log(1.0 + jnp.exp(-jnp.abs(x)))


def _dot(a, b):
    return jnp.dot(a, b, preferred_element_type=F32)


def _dot_nt(a, b):
    return lax.dot_general(a, b, (((1,), (1,)), ((), ())), preferred_element_type=F32)


def _params(sem):
    return pltpu.CompilerParams(dimension_semantics=sem, vmem_limit_bytes=VMEM_LIMIT)


def _ada_kernel(c_ref, w_ref, b_ref, o_ref):
    s = _silu(c_ref[...])
    o_ref[...] = jnp.dot(s, w_ref[...], precision=HIGHEST, preferred_element_type=F32) + b_ref[...]


def _ada(c_pad, w_ada, b_ada):
    d = c_pad.shape[1]
    ncol = w_ada.shape[1]
    tn = d
    return pl.pallas_call(
        _ada_kernel,
        out_shape=jax.ShapeDtypeStruct((c_pad.shape[0], ncol), F32),
        grid=(ncol // tn,),
        in_specs=[pl.BlockSpec((c_pad.shape[0], d), lambda j: (0, 0)),
                  pl.BlockSpec((d, tn), lambda j: (0, j)),
                  pl.BlockSpec((1, tn), lambda j: (0, j))],
        out_specs=pl.BlockSpec((c_pad.shape[0], tn), lambda j: (0, j)),
        compiler_params=_params(("arbitrary",)),
        name="ada",
    )(c_pad, w_ada, b_ada)


def _inproj_kernel(x_ref, nw_ref, sh_ref, sc_ref, w_ref, wab_ref, o_ref, oab_ref):
    x = x_ref[0]
    ms = jnp.mean(x * x, axis=-1, keepdims=True)
    h = x * lax.rsqrt(ms + NORM_EPS) * nw_ref[...]
    h = h * (1.0 + sc_ref[0]) + sh_ref[0]
    hb = h.astype(BF16)
    ncol = o_ref.shape[2]
    step = 1024
    for j in range(ncol // step):
        o_ref[0, :, j * step:(j + 1) * step] = _dot(hb, w_ref[:, j * step:(j + 1) * step]).astype(o_ref.dtype)
    oab_ref[0] = _dot(hb, wab_ref[...])


def _inproj(x, norm_w, shift, scale, w_main, w_ab):
    b, t, d = x.shape
    tm = min(IN_TM, t)
    return pl.pallas_call(
        _inproj_kernel,
        out_shape=(jax.ShapeDtypeStruct((b, t, PROJ_MAIN), BF16),
                   jax.ShapeDtypeStruct((b, t, LANES), F32)),
        grid=(b, t // tm),
        in_specs=[pl.BlockSpec((1, tm, d), lambda i, j: (i, j, 0)),
                  pl.BlockSpec((1, d), lambda i, j: (0, 0)),
                  pl.BlockSpec((1, 1, d), lambda i, j: (i, 0, 0)),
                  pl.BlockSpec((1, 1, d), lambda i, j: (i, 0, 0)),
                  pl.BlockSpec((d, PROJ_MAIN), lambda i, j: (0, 0)),
                  pl.BlockSpec((d, LANES), lambda i, j: (0, 0))],
        out_specs=(pl.BlockSpec((1, tm, PROJ_MAIN), lambda i, j: (i, j, 0)),
                   pl.BlockSpec((1, tm, LANES), lambda i, j: (i, j, 0))),
        compiler_params=_params(("parallel", "arbitrary")),
        name="inproj",
    )(x, norm_w, shift, scale, w_main, w_ab)


def _ret_kernel(pos_ref, q_ref, k_ref, v_ref, g_ref, o_ref, state_ref):
    tb = q_ref.shape[1]
    c = RET_CHUNK
    dh = HEAD_DIM

    @pl.when(pl.program_id(1) == 0)
    def _():
        state_ref[...] = jnp.zeros_like(state_ref)

    pos = pos_ref[0].astype(F32)
    pos_t = jnp.transpose(jnp.broadcast_to(pos, (LANES, tb)))
    lane = lax.broadcasted_iota(jnp.int32, (1, dh), 1)
    half = dh // 2
    fidx = jnp.where(lane >= half, lane - half, lane).astype(F32)
    inv_freq = jnp.power(jnp.float32(ROPE_THETA), -(2.0 * fidx) / dh)
    ang = pos_t * inv_freq
    cos = jnp.cos(ang)
    sin = jnp.where(lane >= half, 1.0, -1.0) * jnp.sin(ang)

    ri = lax.broadcasted_iota(jnp.int32, (c, c), 0)
    ci = lax.broadcasted_iota(jnp.int32, (c, c), 1)
    rel = (ri - ci).astype(F32)
    causal = ri >= ci
    row_f = lax.broadcasted_iota(jnp.int32, (c, dh), 0).astype(F32)

    for h in range(RET_HEADS):
        log_gamma = math.log1p(-(2.0 ** (-5.0 - h)))
        decay = jnp.where(causal, jnp.exp(log_gamma * jnp.where(causal, rel, 0.0)), 0.0)
        q_dec = jnp.exp(log_gamma * (row_f + 1.0))
        k_dec = jnp.exp(log_gamma * (c - 1.0 - row_f))
        chunk_decay = math.exp(log_gamma * c)
        sl = slice(h * dh, (h + 1) * dh)
        qh = q_ref[0, :, sl].astype(F32)
        kh = k_ref[0, :, sl].astype(F32)
        qh = qh * cos + pltpu.roll(qh, half, axis=1) * sin
        kh = (kh * cos + pltpu.roll(kh, half, axis=1) * sin) * (dh ** -0.5)
        state = state_ref[h]
        for n in range(tb // c):
            rs = slice(n * c, (n + 1) * c)
            qc = qh[rs]
            kc = kh[rs]
            vc = v_ref[0, rs, sl]
            scores = _dot_nt(qc.astype(BF16), kc.astype(BF16)) * decay
            inner = _dot(scores.astype(BF16), vc)
            cross = _dot((qc * q_dec).astype(BF16), state.astype(BF16))
            kw = (kc * k_dec).astype(BF16)
            incr = _dot(jnp.transpose(kw.astype(F32)).astype(BF16), vc)
            state = chunk_decay * state + incr
            o = inner + cross
            o = o * lax.rsqrt(jnp.mean(o * o, axis=-1, keepdims=True) + NORM_EPS)
            gate = g_ref[0, rs, sl].astype(F32)
            o_ref[0, rs, sl] = (o * _silu(gate)).astype(o_ref.dtype)
        state_ref[h] = state


def _retention(pos3, proj):
    b, t, _ = proj.shape
    tb = min(RET_TB, t)
    w = RET_WIDTH
    spec = lambda col: pl.BlockSpec((1, tb, w), lambda i, j, col=col: (i, j, col))
    return pl.pallas_call(
        _ret_kernel,
        out_shape=jax.ShapeDtypeStruct((b, t, w), BF16),
        grid=(b, t // tb),
        in_specs=[pl.BlockSpec((1, 1, tb), lambda i, j: (i, 0, j)), spec(0), spec(1), spec(2), spec(3)],
        out_specs=pl.BlockSpec((1, tb, w), lambda i, j: (i, j, 0)),
        scratch_shapes=[pltpu.VMEM((RET_HEADS, HEAD_DIM, HEAD_DIM), F32)],
        compiler_params=_params(("parallel", "arbitrary")),
        name="retention",
    )(pos3, proj, proj, proj, proj)


def _dn_kernel(q_ref, k_ref, v_ref, z_ref, ab_ref, cw_ref, alog_ref, dtb_ref, nw_ref, o_ref,
               state_ref, tail_ref):
    tb = q_ref.shape[1]
    c = DN_CHUNK
    dh = HEAD_DIM
    nh = DN_HEADS
    w = DN_WIDTH
    sr = nh * c

    @pl.when(pl.program_id(1) == 0)
    def _():
        state_ref[...] = jnp.zeros_like(state_ref)
        tail_ref[...] = jnp.zeros_like(tail_ref)

    def conv(idx, x_ref):
        x = x_ref[0].astype(F32)
        xp = jnp.concatenate([tail_ref[idx], x], axis=0)
        cw = cw_ref[:, idx * w:(idx + 1) * w]
        y = x * cw[CONV_WIDTH - 1:CONV_WIDTH]
        for j in range(1, CONV_WIDTH):
            xs = pltpu.roll(xp, j, axis=0)[SUBLANES:]
            y = y + xs * cw[CONV_WIDTH - 1 - j:CONV_WIDTH - j]
        tail_ref[idx] = x[tb - SUBLANES:]
        return _silu(y)

    cq = conv(0, q_ref)
    ck = conv(1, k_ref)
    cv = conv(2, v_ref)

    ab = ab_ref[0]
    g_all = -jnp.exp(alog_ref[...]) * _softplus(ab + dtb_ref[...])
    beta_all = _sigmoid(ab)

    r = lax.broadcasted_iota(jnp.int32, (sr, sr), 0)
    cc = lax.broadcasted_iota(jnp.int32, (sr, sr), 1)
    same = (r >> 6) == (cc >> 6)
    incl = jnp.logical_and(same, r >= cc)
    strict = jnp.logical_and(same, r > cc)
    upper = jnp.logical_and(same, cc > r)
    blk16 = (r >> 4) == (cc >> 4)
    blk32 = (r >> 5) == (cc >> 5)
    eye = (r == cc).astype(F32)
    l_incl = jnp.where(incl, 1.0, 0.0)
    u_strict = jnp.where(upper, 1.0, 0.0)

    def stack(x, n):
        return jnp.concatenate([x[n * c:(n + 1) * c, h * dh:(h + 1) * dh] for h in range(nh)], axis=0)

    def stack_col(x, n, lane0):
        return jnp.concatenate([x[n * c:(n + 1) * c, lane0 + h:lane0 + h + 1] for h in range(nh)], axis=0)

    for n in range(tb // c):
        qs = stack(cq, n)
        ks = stack(ck, n)
        vs = stack(cv, n)
        qs = qs * lax.rsqrt(jnp.sum(qs * qs, axis=-1, keepdims=True) + NORM_EPS) * (dh ** -0.5)
        ks = ks * lax.rsqrt(jnp.sum(ks * ks, axis=-1, keepdims=True) + NORM_EPS)
        g_b = jnp.broadcast_to(stack_col(g_all, n, 0), (sr, dh))
        beta_b = jnp.broadcast_to(stack_col(beta_all, n, nh), (sr, dh))

        gc = jnp.dot(l_incl, g_b, precision=HIGHEST, preferred_element_type=F32)
        g_rest = jnp.dot(u_strict, g_b, precision=HIGHEST, preferred_element_type=F32)
        gc_t = jnp.transpose(gc)
        gc_row = jnp.concatenate([gc_t, gc_t], axis=0)
        gc_col = jnp.concatenate([gc, gc], axis=1)
        decay = jnp.where(incl, jnp.exp(jnp.where(incl, gc_col - gc_row, 0.0)), 0.0)

        kb = ks * beta_b
        ksb = ks.astype(BF16)
        a_mat = jnp.where(strict, _dot_nt(kb.astype(BF16), ksb) * decay, 0.0)
        attn = _dot_nt(qs.astype(BF16), ksb) * decay

        d16 = jnp.where(blk16, a_mat, 0.0)
        d1 = d16.astype(BF16)
        d2 = _dot(d1, d1)
        d2b = d2.astype(BF16)
        d4 = _dot(d2b, d2b)
        d4b = d4.astype(BF16)
        d8b = _dot(d4b, d4b).astype(BF16)
        p = eye - d16
        p = p + _dot(p.astype(BF16), d2b)
        p = p + _dot(p.astype(BF16), d4b)
        p = p + _dot(p.astype(BF16), d8b)
        off32 = jnp.where(jnp.logical_and(blk32, jnp.logical_not(blk16)), a_mat, 0.0).astype(BF16)
        pb = p.astype(BF16)
        p = p - _dot(pb, _dot(off32, pb).astype(BF16))
        off64 = jnp.where(blk32, 0.0, a_mat).astype(BF16)
        pb = p.astype(BF16)
        t_mat = p - _dot(pb, _dot(off64, pb).astype(BF16))

        e_gc = jnp.exp(gc)
        rhs = jnp.concatenate([vs * beta_b, kb * e_gc], axis=1).astype(BF16)
        uw = _dot(t_mat.astype(BF16), rhs)
        u = uw[:, :dh]
        wm = uw[:, dh:].astype(BF16)
        qg = (qs * e_gc).astype(BF16)
        kt = (ks * jnp.exp(g_rest)).astype(BF16)

        v_new = []
        o1 = []
        for h in range(nh):
            hs = slice(h * c, (h + 1) * c)
            state = state_ref[h]
            sb = state.astype(BF16)
            vn = u[hs] - _dot(wm[hs], sb)
            o1.append(_dot(qg[hs], sb))
            g_last = jnp.exp(gc[h * c + c - 1:h * c + c, :])
            kt_t = jnp.transpose(jnp.concatenate([kt[hs].astype(F32), jnp.zeros((c, dh), F32)], axis=0))
            vn_pad = jnp.concatenate([vn, jnp.zeros((c, dh), F32)], axis=0)
            state_ref[h] = state * g_last + _dot(kt_t.astype(BF16), vn_pad.astype(BF16))
            v_new.append(vn)
        o2 = _dot(attn.astype(BF16), jnp.concatenate(v_new, axis=0).astype(BF16))
        for h in range(nh):
            o = o1[h] + o2[h * c:(h + 1) * c]
            o = o * lax.rsqrt(jnp.mean(o * o, axis=-1, keepdims=True) + NORM_EPS) * nw_ref[...]
            z = z_ref[0, n * c:(n + 1) * c, h * dh:(h + 1) * dh].astype(F32)
            o_ref[0, n * c:(n + 1) * c, h * dh:(h + 1) * dh] = (o * _silu(z)).astype(o_ref.dtype)


def _deltanet(proj, proj_ab, conv_w, alog_row, dtb_row, dn_norm_w):
    b, t, _ = proj.shape
    tb = min(DN_TB, t)
    w = DN_WIDTH
    spec = lambda col: pl.BlockSpec((1, tb, w), lambda i, j, col=col: (i, j, col))
    const = lambda shape: pl.BlockSpec(shape, lambda i, j: (0,) * len(shape))
    return pl.pallas_call(
        _dn_kernel,
        out_shape=jax.ShapeDtypeStruct((b, t, w), BF16),
        grid=(b, t // tb),
        in_specs=[spec(4), spec(5), spec(6), spec(7),
                  pl.BlockSpec((1, tb, LANES), lambda i, j: (i, j, 0)),
                  const((CONV_WIDTH, 3 * w)), const((1, LANES)), const((1, LANES)), const((1, HEAD_DIM))],
        out_specs=pl.BlockSpec((1, tb, w), lambda i, j: (i, j, 0)),
        scratch_shapes=[pltpu.VMEM((DN_HEADS, HEAD_DIM, HEAD_DIM), F32),
                        pltpu.VMEM((3, SUBLANES, w), F32)],
        compiler_params=_params(("parallel", "arbitrary")),
        name="deltanet",
    )(proj, proj, proj, proj, proj_ab, conv_w, alog_row, dtb_row, dn_norm_w)


def _outproj_kernel(ret_ref, dn_ref, x_ref, wo_ref, g1_ref, nw_ref, sh_ref, sc_ref, wr_ref, br_ref,
                    x1_ref, h2_ref, ri_ref, rg_ref, gcol_ref, cnt_ref, carry_ref):
    tm = x_ref.shape[1]
    d = x_ref.shape[2]
    half = wo_ref.shape[0] // 2

    @pl.when(jnp.logical_and(pl.program_id(0) == 0, pl.program_id(1) == 0))
    def _():
        carry_ref[...] = jnp.zeros_like(carry_ref)

    mix = _dot(ret_ref[0], wo_ref[:half]) + _dot(dn_ref[0], wo_ref[half:])
    x1 = x_ref[0] + g1_ref[0] * mix
    x1_ref[0] = x1
    ms = jnp.mean(x1 * x1, axis=-1, keepdims=True)
    h2 = x1 * lax.rsqrt(ms + NORM_EPS) * nw_ref[...]
    h2 = h2 * (1.0 + sc_ref[0]) + sh_ref[0]
    for j in range(d // LANES):
        h2_ref[pl.ds(j, tm, stride=SUBLANES), :] = h2[:, j * LANES:(j + 1) * LANES]

    logits = jnp.dot(h2, wr_ref[...], precision=HIGHEST, preferred_element_type=F32) + br_ref[...]
    lt = jnp.transpose(logits)
    sub = lax.broadcasted_iota(jnp.int32, (SUBLANES, tm), 0).astype(F32)
    neg = jnp.float32(-1e30)

    lg = jnp.where(sub < N_GROUPS, lt[0:SUBLANES], neg)
    mg = jnp.max(lg, axis=0, keepdims=True)
    gi = jnp.min(jnp.where(lg == mg, sub, SUBLANES), axis=0, keepdims=True)
    p_group = 1.0 / jnp.sum(jnp.exp(lg - mg), axis=0, keepdims=True)

    le = jnp.zeros((SUBLANES, tm), F32)
    for g in range(N_GROUPS):
        le = le + jnp.where(gi == g, lt[EXPERT_COL0 + g * SUBLANES:EXPERT_COL0 + (g + 1) * SUBLANES], 0.0)
    m1 = jnp.max(le, axis=0, keepdims=True)
    i1 = jnp.min(jnp.where(le == m1, sub, SUBLANES), axis=0, keepdims=True)
    le2 = jnp.where(sub == i1, neg, le)
    m2 = jnp.max(le2, axis=0, keepdims=True)
    i2 = jnp.min(jnp.where(le2 == m2, sub, SUBLANES), axis=0, keepdims=True)
    e2 = jnp.exp(m2 - m1)
    gate0 = p_group / (1.0 + e2)
    gate1 = p_group * e2 / (1.0 + e2)
    ex0 = gi * EXPERTS_PER_GROUP + i1
    ex1 = gi * EXPERTS_PER_GROUP + i2

    eid = lax.broadcasted_iota(jnp.int32, (N_EXPERTS, tm), 0).astype(F32)
    oh0 = jnp.where(eid == ex0, 1.0, 0.0)
    oh1 = jnp.where(eid == ex1, 1.0, 0.0)
    oh = oh0 + oh1
    tr = lax.broadcasted_iota(jnp.int32, (tm, tm), 0)
    tc = lax.broadcasted_iota(jnp.int32, (tm, tm), 1)
    upper_incl = jnp.where(tr <= tc, 1.0, 0.0).astype(BF16)
    before = _dot(oh.astype(BF16), upper_incl) - oh + carry_ref[...]
    rank0 = jnp.sum(oh0 * before, axis=0, keepdims=True)
    rank1 = jnp.sum(oh1 * before, axis=0, keepdims=True)
    carry = carry_ref[...] + jnp.sum(oh, axis=1, keepdims=True)
    carry_ref[...] = carry
    cnt_ref[...] = carry[:, 0:LANES]

    zf = jnp.zeros((1, tm), F32)
    ri_ref[...] = jnp.concatenate([ex0, ex1, rank0, rank1, zf, zf, zf, zf], axis=0).astype(jnp.int32)
    gates = jnp.concatenate([gate0, gate1, zf, zf, zf, zf, zf, zf], axis=0)
    rg_ref[...] = gates
    gpad = jnp.concatenate([gates, jnp.zeros((LANES - SUBLANES, tm), F32)], axis=0)
    gcol_ref[...] = jnp.transpose(gpad)


def _outproj(ret, dn, x, w_out, gate1, norm_w, shift, scale, w_route, b_route):
    b, t, d = x.shape
    tm = min(OUT_TM, t)
    n = b * t
    nt = t // tm
    const = lambda shape: pl.BlockSpec(shape, lambda i, j: (0,) * len(shape))
    per_b = pl.BlockSpec((1, 1, d), lambda i, j: (i, 0, 0))
    return pl.pallas_call(
        _outproj_kernel,
        out_shape=(jax.ShapeDtypeStruct((b, t, d), F32),
                   jax.ShapeDtypeStruct((n * SUBLANES, LANES), F32),
                   jax.ShapeDtypeStruct((SUBLANES, n), jnp.int32),
                   jax.ShapeDtypeStruct((SUBLANES, n), F32),
                   jax.ShapeDtypeStruct((n, LANES), F32),
                   jax.ShapeDtypeStruct((N_EXPERTS, LANES), F32)),
        grid=(b, nt),
        in_specs=[pl.BlockSpec((1, tm, RET_WIDTH), lambda i, j: (i, j, 0)),
                  pl.BlockSpec((1, tm, DN_WIDTH), lambda i, j: (i, j, 0)),
                  pl.BlockSpec((1, tm, d), lambda i, j: (i, j, 0)),
                  const(w_out.shape), per_b, const((1, d)), per_b, per_b,
                  const((d, ROUTE_COLS)), const((1, ROUTE_COLS))],
        out_specs=(pl.BlockSpec((1, tm, d), lambda i, j: (i, j, 0)),
                   pl.BlockSpec((tm * SUBLANES, LANES), lambda i, j: (i * nt + j, 0)),
                   pl.BlockSpec((SUBLANES, tm), lambda i, j: (0, i * nt + j)),
                   pl.BlockSpec((SUBLANES, tm), lambda i, j: (0, i * nt + j)),
                   pl.BlockSpec((tm, LANES), lambda i, j: (i * nt + j, 0)),
                   const((N_EXPERTS, LANES))),
        scratch_shapes=[pltpu.VMEM((N_EXPERTS, tm), F32)],
        compiler_params=_params(("arbitrary", "arbitrary")),
        name="outproj_router",
    )(ret, dn, x, w_out, gate1, norm_w, shift, scale, w_route, b_route)


def _drain(sem, like_ref, n_tiles):
    grp = min(ROWS_WAIT_GROUP, n_tiles)
    for _ in range(n_tiles // grp):
        pltpu.make_async_copy(like_ref.at[pl.ds(0, grp * SUBLANES)],
                              like_ref.at[pl.ds(0, grp * SUBLANES)], sem).wait()


def _dispatch_kernel(idx_ref, h_ref, xs_ref, zero_ref, sem, *, tok_steps):
    cht = h_ref.shape[0] // SUBLANES

    def dst_tile(i):
        return xs_ref.at[pl.ds(idx_ref[0, 0, i] * SUBLANES, SUBLANES)]

    @pl.when(pl.program_id(0) < tok_steps)
    def _():
        def body(j, carry):
            for u in range(ROWS_UNROLL):
                i = j * ROWS_UNROLL + u
                src = h_ref.at[pl.ds(i * SUBLANES, SUBLANES)]
                pltpu.make_async_copy(src, dst_tile(i), sem).start()
                pltpu.make_async_copy(src, dst_tile(cht + i), sem).start()
            return carry
        lax.fori_loop(0, cht // ROWS_UNROLL, body, 0)

    @pl.when(pl.program_id(0) >= tok_steps)
    def _():
        zero_ref[...] = jnp.zeros_like(zero_ref)

        def body(j, carry):
            for u in range(ROWS_UNROLL):
                pltpu.make_async_copy(zero_ref, dst_tile(j * ROWS_UNROLL + u), sem).start()
            return carry
        lax.fori_loop(0, 2 * cht // ROWS_UNROLL, body, 0)

    _drain(sem, h_ref, 2 * cht)


def _dispatch(pos0, pos1, pad_dst, h2r):
    n = pos0.shape[0]
    n_pad = pad_dst.shape[0]
    cht = min(ROWS_CH // 2, n)
    tok_steps = n // cht
    idx = jnp.concatenate([jnp.concatenate([pos0.reshape(tok_steps, cht), pos1.reshape(tok_steps, cht)], axis=1),
                           pad_dst.reshape(n_pad // (2 * cht), 2 * cht)], axis=0)
    steps = idx.shape[0]
    return pl.pallas_call(
        functools.partial(_dispatch_kernel, tok_steps=tok_steps),
        out_shape=jax.ShapeDtypeStruct(((2 * n + n_pad) * SUBLANES, LANES), h2r.dtype),
        grid=(steps,),
        in_specs=[pl.BlockSpec((1, 1, 2 * cht), lambda i: (i, 0, 0), memory_space=pltpu.SMEM),
                  pl.BlockSpec((cht * SUBLANES, LANES), lambda i: (jnp.minimum(i, tok_steps - 1), 0))],
        out_specs=pl.BlockSpec(memory_space=pl.ANY),
        scratch_shapes=[pltpu.VMEM((SUBLANES, LANES), h2r.dtype), pltpu.SemaphoreType.DMA],
        compiler_params=_params(("arbitrary",)),
        name="dispatch",
    )(idx.reshape(steps, 1, 2 * cht), h2r)


def _gmm_kernel(te_ref, x_ref, w1_ref, w3_ref, w2_ref, y_ref):
    tm = x_ref.shape[0] // SUBLANES
    d = w1_ref.shape[1]
    nblk = d // LANES
    x = jnp.concatenate([x_ref[pl.ds(j, tm, stride=SUBLANES), :] for j in range(nblk)], axis=1).astype(BF16)
    h1 = _dot(x, w1_ref[0])
    h3 = _dot(x, w3_ref[0])
    hid = (_silu(h1) * h3).astype(BF16)
    y = _dot(hid, w2_ref[0])
    for j in range(nblk):
        y_ref[pl.ds(j, tm, stride=SUBLANES), :] = y[:, j * LANES:(j + 1) * LANES]


def _gmm(tile_expert, xs, w1, w3, w2):
    n_rows = xs.shape[0] // SUBLANES
    tm = GMM_TM
    ne, d, de = w1.shape
    return pl.pallas_call(
        _gmm_kernel,
        out_shape=jax.ShapeDtypeStruct(xs.shape, F32),
        grid_spec=pltpu.PrefetchScalarGridSpec(
            num_scalar_prefetch=1,
            grid=(n_rows // tm,),
            in_specs=[pl.BlockSpec((tm * SUBLANES, LANES), lambda i, te: (i, 0)),
                      pl.BlockSpec((1, d, de), lambda i, te: (te[i], 0, 0)),
                      pl.BlockSpec((1, d, de), lambda i, te: (te[i], 0, 0)),
                      pl.BlockSpec((1, de, d), lambda i, te: (te[i], 0, 0))],
            out_specs=pl.BlockSpec((tm * SUBLANES, LANES), lambda i, te: (i, 0))),
        compiler_params=_params(("arbitrary",)),
        name="gmm",
    )(tile_expert, xs, w1, w3, w2)


def _final_kernel(idx_ref, x1_ref, ys_ref, gcol_ref, g2_ref, fw_ref, o_ref, y_ref, sem, *, normalize):
    tm = x1_ref.shape[1]
    d = x1_ref.shape[2]
    nblk = d // LANES

    def body(j, carry):
        for u in range(ROWS_UNROLL):
            i = j * ROWS_UNROLL + u
            for s in range(2):
                src = ys_ref.at[pl.ds(idx_ref[0, 0, s * tm + i] * SUBLANES, SUBLANES)]
                pltpu.make_async_copy(src, y_ref.at[pl.ds((2 * i + s) * SUBLANES, SUBLANES)], sem).start()
        return carry
    lax.fori_loop(0, tm // ROWS_UNROLL, body, 0)
    _drain(sem, y_ref, 2 * tm)

    y0 = jnp.concatenate([y_ref[pl.ds(j, tm, stride=2 * SUBLANES), :] for j in range(nblk)], axis=1)
    y1 = jnp.concatenate([y_ref[pl.ds(SUBLANES + j, tm, stride=2 * SUBLANES), :] for j in range(nblk)], axis=1)
    gc = gcol_ref[...]
    ffn = y0 * gc[:, 0:1] + y1 * gc[:, 1:2]
    x2 = x1_ref[0] + g2_ref[0] * ffn
    if normalize:
        x2 = x2 * lax.rsqrt(jnp.mean(x2 * x2, axis=-1, keepdims=True) + NORM_EPS) * fw_ref[...]
    o_ref[0] = x2


def _final(x1, ys, pos0, pos1, gcol, gate2, final_w, normalize):
    b, t, d = x1.shape
    tm = min(FIN_TM, t)
    nt = t // tm
    steps = b * nt
    idx = jnp.concatenate([pos0.reshape(steps, tm), pos1.reshape(steps, tm)], axis=1).reshape(steps, 1, 2 * tm)
    return pl.pallas_call(
        functools.partial(_final_kernel, normalize=normalize),
        out_shape=jax.ShapeDtypeStruct((b, t, d), F32),
        grid=(b, nt),
        in_specs=[pl.BlockSpec((1, 1, 2 * tm), lambda i, j: (i * nt + j, 0, 0), memory_space=pltpu.SMEM),
                  pl.BlockSpec((1, tm, d), lambda i, j: (i, j, 0)),
                  pl.BlockSpec(memory_space=pl.ANY),
                  pl.BlockSpec((tm, LANES), lambda i, j: (i * nt + j, 0)),
                  pl.BlockSpec((1, 1, d), lambda i, j: (i, 0, 0)),
                  pl.BlockSpec((1, d), lambda i, j: (0, 0))],
        out_specs=pl.BlockSpec((1, tm, d), lambda i, j: (i, j, 0)),
        scratch_shapes=[pltpu.VMEM((tm * 2 * SUBLANES, LANES), F32), pltpu.SemaphoreType.DMA],
        compiler_params=_params(("parallel", "arbitrary")),
        name="final",
    )(idx, x1, ys, gcol, gate2, final_w)


def _pad_lanes(v, lane0=0):
    return jnp.zeros((1, LANES), F32).at[0, lane0:lane0 + v.shape[0]].set(v.astype(F32))


def kernel(x, c, positions, w_ada, b_ada, norm1_w, w_in, conv_w, a_log, dt_bias, dn_norm_w, w_out,
           norm2_w, w_group, b_group, w_expert, b_expert, w1, w3, w2, final_norm_w):
    b, t, d = x.shape
    n = b * t
    depth = w_ada.shape[0]
    pos3 = positions.reshape(b, 1, t)
    c_pad = jnp.zeros((SUBLANES, d), F32).at[:b].set(c)

    for l in range(depth):
        mod = _ada(c_pad, w_ada[l], b_ada[l].reshape(1, -1))[:b]
        shift1, scale1, gate1, shift2, scale2, gate2 = [m.reshape(b, 1, d) for m in jnp.split(mod, N_MOD, axis=-1)]

        w_main = w_in[l][:, :PROJ_MAIN].astype(BF16)
        w_ab = jnp.zeros((d, LANES), F32).at[:, :2 * DN_HEADS].set(w_in[l][:, PROJ_MAIN:]).astype(BF16)
        proj, proj_ab = _inproj(x, norm1_w[l].reshape(1, d), shift1, scale1, w_main, w_ab)

        ret = _retention(pos3, proj)
        dn = _deltanet(proj, proj_ab, conv_w[l], _pad_lanes(a_log[l]), _pad_lanes(dt_bias[l]),
                       dn_norm_w[l].reshape(1, HEAD_DIM))

        w_route = (jnp.zeros((d, ROUTE_COLS), F32).at[:, :N_GROUPS].set(w_group[l])
                   .at[:, EXPERT_COL0:EXPERT_COL0 + N_EXPERTS].set(w_expert[l]))
        b_route = (jnp.zeros((1, ROUTE_COLS), F32).at[0, :N_GROUPS].set(b_group[l])
                   .at[0, EXPERT_COL0:EXPERT_COL0 + N_EXPERTS].set(b_expert[l]))
        x1, h2r, ri, _, gcol, cnt = _outproj(ret, dn, x, w_out[l].astype(BF16), gate1, norm2_w[l].reshape(1, d),
                                             shift2, scale2, w_route, b_route)

        counts = cnt[:, 0].astype(jnp.int32)
        padded = ((counts + GMM_TM - 1) // GMM_TM) * GMM_TM
        ends = jnp.cumsum(padded)
        starts = ends - padded
        n_pad = N_EXPERTS * GMM_TM
        n_rows = 2 * n + n_pad
        pos0 = starts[ri[0]] + ri[2]
        pos1 = starts[ri[1]] + ri[3]
        pad_cnt = (padded - counts).at[N_EXPERTS - 1].add(n_rows - ends[N_EXPERTS - 1])
        pad_end = jnp.cumsum(pad_cnt)
        q = jnp.arange(n_pad, dtype=jnp.int32)
        q_exp = jnp.sum(q[:, None] >= pad_end[None, :], axis=1)
        pad_dst = (starts + counts)[q_exp] + q - (pad_end - pad_cnt)[q_exp]
        xs = _dispatch(pos0, pos1, pad_dst.astype(jnp.int32), h2r)

        tile_start = jnp.arange(n_rows // GMM_TM, dtype=jnp.int32) * GMM_TM
        tile_expert = jnp.minimum(jnp.sum(tile_start[:, None] >= ends[None, :], axis=1), N_EXPERTS - 1).astype(jnp.int32)
        ys = _gmm(tile_expert, xs, w1[l].astype(BF16), w3[l].astype(BF16), w2[l].astype(BF16))

        x = _final(x1, ys, pos0, pos1, gcol, gate2, final_norm_w.reshape(1, d), normalize=(l == depth - 1))
    return x
```
